```python
import math
import jax, jax.numpy as jnp
from jax import lax
import numpy as np

D_MODEL = 1024
BATCH = 2
SEQ = 16384
DEPTH = 4

D_PLE = 256
SSD_WIDTH = 512
SSD_HEAD_DIM = 64
SSD_HEADS = SSD_WIDTH // SSD_HEAD_DIM
SSD_GROUPS = 2
SSD_STATE = 128
SSD_CONV = 4
SSD_CHUNK = 128
SSD_XBC = SSD_WIDTH + 2 * SSD_GROUPS * SSD_STATE
POOL_WINDOWS = (2, 4, 8, 16)
POOL_WIDTH = D_MODEL - SSD_WIDTH
POOL_GROUP = POOL_WIDTH // len(POOL_WINDOWS)
D_MIX = SSD_WIDTH + POOL_WIDTH
D_IN_PROJ = SSD_WIDTH + SSD_XBC + SSD_HEADS + POOL_WIDTH
D_FF = 2816
FFN_CONV = 3
EPS = 1e-6

kernel_name = 'hymba_ssd_pool_convffn_ple'


def rmsnorm(x, g):
    xf = x.astype(jnp.float32)
    y = xf * lax.rsqrt(jnp.mean(xf * xf, axis=-1, keepdims=True) + EPS)
    return (y * g.astype(jnp.float32)).astype(x.dtype)


def causal_dwconv(x, w, b):
    k_taps = w.shape[0]
    s = x.shape[1]
    xp = jnp.pad(x, ((0, 0), (k_taps - 1, 0), (0, 0)))
    out = b + xp[:, 0:s] * w[0]
    for k in range(1, k_taps):
        out = out + xp[:, k:k + s] * w[k]
    return out


def segsum_exp(a):
    q = a.shape[-1]
    cs = jnp.cumsum(a, axis=-1)
    diff = cs[..., :, None] - cs[..., None, :]
    mask = jnp.tril(jnp.ones((q, q), dtype=bool))
    return jnp.exp(jnp.where(mask, diff, -jnp.inf))


def ssd_scan(x, dt, A, B, C):
    b, l, h, p = x.shape
    g, n = B.shape[-2:]
    e = h // g
    q = SSD_CHUNK
    c = l // q
    X = (x * dt[..., None]).reshape(b, c, q, g, e, p)
    a = (dt * A).reshape(b, c, q, g, e).transpose(0, 3, 4, 1, 2)
    Bc = B.reshape(b, c, q, g, n)
    Cc = C.reshape(b, c, q, g, n)
    a_cs = jnp.cumsum(a, axis=-1)
    CB = jnp.einsum('bclgn,bcsgn->bgcls', Cc, Bc)
    M = CB[:, :, None] * segsum_exp(a)
    y_diag = jnp.einsum('bgecls,bcsgep->bclgep', M, X)
    decay_states = jnp.exp(a_cs[..., -1:] - a_cs).transpose(0, 3, 4, 1, 2)
    states = jnp.einsum('bclgn,bclgep->bcgepn', Bc, X * decay_states[..., None])
    chunk_decay = jnp.exp(a_cs[..., -1])

    def step(s, inp):
        dec, st = inp
        return s * dec[..., None, None] + st, s

    init = jnp.zeros((b, g, e, p, n), jnp.float32)
    _, prev = lax.scan(step, init, (jnp.moveaxis(chunk_decay, -1, 0), jnp.moveaxis(states, 1, 0)))
    y_off = jnp.einsum('bclgn,cbgepn->bclgep', Cc, prev) * jnp.exp(a_cs).transpose(0, 3, 4, 1, 2)[..., None]
    return (y_diag + y_off).reshape(b, l, h, p)


def ssd_branch(z, xbc, dt_raw, conv_w, conv_b, dt_bias, a_log, d_skip, norm_g):
    b, s, _ = z.shape
    f32 = jnp.float32
    xbc = jax.nn.silu(causal_dwconv(xbc, conv_w, conv_b))
    xs, Bm, Cm = jnp.split(xbc, [SSD_WIDTH, SSD_WIDTH + SSD_GROUPS * SSD_STATE], axis=-1)
    dt = jax.nn.softplus(dt_raw.astype(f32) + dt_bias.astype(f32))
    A = -jnp.exp(a_log.astype(f32))
    xh = xs.astype(f32).reshape(b, s, SSD_HEADS, SSD_HEAD_DIM)
    y = ssd_scan(xh, dt, A,
                 Bm.astype(f32).reshape(b, s, SSD_GROUPS, SSD_STATE),
                 Cm.astype(f32).reshape(b, s, SSD_GROUPS, SSD_STATE))
    y = y + xh * d_skip.astype(f32)[:, None]
    y = y.reshape(b, s, SSD_WIDTH) * jax.nn.silu(z.astype(f32))
    gs = SSD_WIDTH // SSD_GROUPS
    y = rmsnorm(y.reshape(b, s, SSD_GROUPS, gs), norm_g.reshape(SSD_GROUPS, gs))
    return y.reshape(b, s, SSD_WIDTH).astype(z.dtype)


def pool_branch(u, pool_w, pool_scale):
    b, s, _ = u.shape
    f32 = jnp.float32
    uf = u.astype(f32).reshape(b, s, len(POOL_WINDOWS), POOL_GROUP)
    cs = jnp.cumsum(uf, axis=1)
    pos = jnp.arange(1, s + 1, dtype=f32)
    outs = []
    for gi, w in enumerate(POOL_WINDOWS):
        c = cs[:, :, gi]
        lag = jnp.pad(c, ((0, 0), (w, 0), (0, 0)))[:, :s]
        mean = (c - lag) / jnp.minimum(pos, float(w))[:, None]
        outs.append(mean - uf[:, :, gi])
    pooled = jnp.stack(outs, axis=2)
    mixed = jnp.einsum('bsgc,gcd->bsgd', pooled, pool_w.astype(f32))
    return (mixed.reshape(b, s, POOL_WIDTH) * pool_scale.astype(f32)).astype(u.dtype)


def conv_ffn(h, w_up, conv_w, conv_b, w_down):
    up = causal_dwconv(h @ w_up, conv_w, conv_b)
    gate, val = jnp.split(up, 2, axis=-1)
    return (jax.nn.gelu(gate) * val) @ w_down


def setup_inputs(seed: int = 0) -> dict:
    key = jax.random.key(seed)
    ks = jax.random.split(key, 24)
    f32 = jnp.float32
    L = DEPTH

    def nrm(k, shape, scale):
        return jax.random.normal(k, shape, f32) * scale

    def gain(k, shape):
        return 1.0 + 0.02 * jax.random.normal(k, shape, f32)

    dt0 = jnp.exp(jax.random.uniform(ks[5], (L, SSD_HEADS), f32, math.log(1e-3), math.log(1e-1)))
    return {
        'x': jax.random.normal(ks[0], (BATCH, SEQ, D_MODEL), f32),
        'p': jax.random.normal(ks[1], (DEPTH, BATCH, SEQ, D_PLE), f32),
        'mix_norm_g': gain(ks[2], (L, D_MODEL)),
        'w_in': nrm(ks[3], (L, D_MODEL, D_IN_PROJ), D_MODEL ** -0.5),
        'ssd_conv_w': nrm(ks[4], (L, SSD_CONV, SSD_XBC), SSD_CONV ** -0.5),
        'ssd_conv_b': nrm(ks[6], (L, SSD_XBC), 0.02),
        'ssd_dt_bias': dt0 + jnp.log(-jnp.expm1(-dt0)),
        'ssd_a_log': jnp.log(jax.random.uniform(ks[7], (L, SSD_HEADS), f32, 1.0, 16.0)),
        'ssd_d': gain(ks[8], (L, SSD_HEADS)),
        'ssd_norm_g': gain(ks[9], (L, SSD_WIDTH)),
        'pool_w': nrm(ks[10], (L, len(POOL_WINDOWS), POOL_GROUP, POOL_GROUP), POOL_GROUP ** -0.5),
        'pool_scale': gain(ks[11], (L, POOL_WIDTH)),
        'w_out': nrm(ks[12], (L, D_MIX, D_MODEL), D_MIX ** -0.5),
        'ffn_norm_g': gain(ks[13], (L, D_MODEL)),
        'ffn_w_up': nrm(ks[14], (L, D_MODEL, 2 * D_FF), D_MODEL ** -0.5),
        'ffn_conv_w': nrm(ks[15], (L, FFN_CONV, 2 * D_FF), FFN_CONV ** -0.5),
        'ffn_conv_b': nrm(ks[16], (L, 2 * D_FF), 0.02),
        'ffn_w_down': nrm(ks[17], (L, D_FF, D_MODEL), D_FF ** -0.5),
        'ple_norm_g': gain(ks[18], (L, D_MODEL)),
        'ple_w_gate': nrm(ks[19], (L, D_MODEL, D_MODEL), D_MODEL ** -0.5),
        'ple_w_proj': nrm(ks[20], (L, D_PLE, D_MODEL), D_PLE ** -0.5),
        'final_norm_g': gain(ks[21], (D_MODEL,)),
    }


def reference(x, p, mix_norm_g, w_in, ssd_conv_w, ssd_conv_b, ssd_dt_bias, ssd_a_log, ssd_d, ssd_norm_g,
              pool_w, pool_scale, w_out, ffn_norm_g, ffn_w_up, ffn_conv_w, ffn_conv_b, ffn_w_down,
              ple_norm_g, ple_w_gate, ple_w_proj, final_norm_g):
    h = x
    splits = [SSD_WIDTH, SSD_WIDTH + SSD_XBC, SSD_WIDTH + SSD_XBC + SSD_HEADS]
    for i in range(DEPTH):
        proj = rmsnorm(h, mix_norm_g[i]) @ w_in[i]
        z, xbc, dt_raw, u = jnp.split(proj, splits, axis=-1)
        y_ssd = ssd_branch(z, xbc, dt_raw, ssd_conv_w[i], ssd_conv_b[i], ssd_dt_bias[i],
                           ssd_a_log[i], ssd_d[i], ssd_norm_g[i])
        y_pool = pool_branch(u, pool_w[i], pool_scale[i])
        h = h + jnp.concatenate([y_ssd, y_pool], axis=-1) @ w_out[i]
        h = h + conv_ffn(rmsnorm(h, ffn_norm_g[i]), ffn_w_up[i], ffn_conv_w[i], ffn_conv_b[i], ffn_w_down[i])
        h = h + (p[i] @ ple_w_proj[i]) * jax.nn.sigmoid(rmsnorm(h, ple_norm_g[i]) @ ple_w_gate[i])
    return rmsnorm(h, final_norm_g)
```

```python
import functools
import math

import jax
import jax.numpy as jnp
from jax import lax
from jax.experimental import pallas as pl
from jax.experimental.pallas import tpu as pltpu

F32 = jnp.float32
BF16 = jnp.bfloat16

D_MODEL = 1024
D_PLE = 256
SSD_WIDTH = 512
SSD_HEAD_DIM = 64
SSD_HEADS = 8
SSD_GROUPS = 2
SSD_STATE = 128
SSD_CONV = 4
SSD_CHUNK = 128
SSD_XBC = SSD_WIDTH + 2 * SSD_GROUPS * SSD_STATE
GROUP_WIDTH = SSD_WIDTH // SSD_GROUPS
HEADS_PER_GROUP = SSD_HEADS // SSD_GROUPS
POOL_WINDOWS = (2, 4, 8, 16)
POOL_WIDTH = 512
POOL_GROUP = 128
D_FF = 2816
FFN_CONV = 3
EPS = 1e-6

LANES = 128
SUBLANES = 8
DT_PAD = LANES
D_PROJ = SSD_WIDTH + SSD_XBC + POOL_WIDTH + DT_PAD
POOL_HALO = 16
FF_BLOCK = 256
SEQ_TILE = 512
VMEM_LIMIT = 56 * 1024 * 1024


def _rms(x, g):
    ms = jnp.mean(x * x, axis=-1, keepdims=True)
    return x * lax.rsqrt(ms + EPS) * g


def _sigmoid(x):
    return 1.0 / (1.0 + jnp.exp(-x))


def _softplus(x):
    return jnp.maximum(x, 0.0) + jnp.log(1.0 + jnp.exp(-jnp.abs(x)))


def _gelu_tanh(x):
    c = math.sqrt(2.0 / math.pi)
    return x * (0.5 * (1.0 + jnp.tanh(c * (x + 0.044715 * (x * x * x)))))


def _shift_rows(x, prev, k):
    r = pltpu.roll(x, k, axis=0)
    pr = pltpu.roll(prev, k, axis=0)
    row = lax.broadcasted_iota(jnp.int32, pr.shape, 0)
    first = jnp.where(row < k, pr, r[0:SUBLANES])
    return jnp.concatenate([first, r[SUBLANES:]], axis=0)


def _causal_conv(x, prev, w, b):
    taps = w.shape[0]
    acc = b + x * w[taps - 1:taps]
    for j in range(1, taps):
        acc = acc + _shift_rows(x, prev, j) * w[taps - 1 - j:taps - j]
    return acc


def _mixer_kernel(h_ref, g_ref, win_ref, cw_ref, cb_ref, dtb_ref, alog_ref, dskip_ref, ng_ref,
                  pw_ref, ps_ref, wout_ref, o_ref,
                  state_ref, xhalo_ref, uhalo_ref, xc_ref, dt_ref, y_ref):
    ts = h_ref.shape[0]
    i = pl.program_id(1)

    @pl.when(i == 0)
    def _():
        state_ref[...] = jnp.zeros_like(state_ref)
        xhalo_ref[...] = jnp.zeros_like(xhalo_ref)
        uhalo_ref[...] = jnp.zeros_like(uhalo_ref)

    h = h_ref[...]
    hn = _rms(h, g_ref[...]).astype(BF16)
    proj = jnp.dot(hn, win_ref[...], preferred_element_type=F32)
    z = proj[:, 0:SSD_WIDTH]
    xbc = proj[:, SSD_WIDTH:SSD_WIDTH + SSD_XBC]
    u = proj[:, SSD_WIDTH + SSD_XBC:SSD_WIDTH + SSD_XBC + POOL_WIDTH]
    dt_raw = proj[:, D_PROJ - DT_PAD:D_PROJ]

    xprev = xhalo_ref[...]
    xhalo_ref[...] = xbc[ts - SUBLANES:ts]
    xconv = _causal_conv(xbc, xprev, cw_ref[...], cb_ref[...])
    xc_ref[...] = xconv * _sigmoid(xconv)
    dt_ref[...] = _softplus(dt_raw + dtb_ref[...])

    a_row = -jnp.exp(alog_ref[...])
    rows = lax.broadcasted_iota(jnp.int32, (SSD_CHUNK, SSD_CHUNK), 0)
    cols = lax.broadcasted_iota(jnp.int32, (SSD_CHUNK, SSD_CHUNK), 1)
    causal = rows >= cols
    tri = causal.astype(F32)
    lane_g = lax.broadcasted_iota(jnp.int32, (SSD_CHUNK, GROUP_WIDTH), 1)
    low_half = cols < SSD_HEAD_DIM

    def chunk_body(c, carry):
        r0 = pl.multiple_of(c * SSD_CHUNK, SSD_CHUNK)
        xcc = xc_ref[pl.ds(r0, SSD_CHUNK), :]
        dtc = dt_ref[pl.ds(r0, SSD_CHUNK), :]
        ccol = jnp.dot(tri, dtc * a_row, precision=lax.Precision.HIGHEST,
                       preferred_element_type=F32)
        c_t = ccol.T
        for g in range(SSD_GROUPS):
            xg = xcc[:, g * GROUP_WIDTH:(g + 1) * GROUP_WIDTH]
            b_off = SSD_WIDTH + g * SSD_STATE
            c_off = SSD_WIDTH + SSD_GROUPS * SSD_STATE + g * SSD_STATE
            bg = xcc[:, b_off:b_off + SSD_STATE].astype(BF16)
            cg = xcc[:, c_off:c_off + SSD_STATE].astype(BF16)
            cb = lax.dot_general(cg, bg, (((1,), (1,)), ((), ())), preferred_element_type=F32)
            bc_c, bc_dt, m_blocks = [], [], []
            for hh in range(HEADS_PER_GROUP):
                hd = g * HEADS_PER_GROUP + hh
                bcc = jnp.broadcast_to(ccol[:, hd:hd + 1], (SSD_CHUNK, SSD_CHUNK))
                bc_c.append(bcc)
                bc_dt.append(jnp.broadcast_to(dtc[:, hd:hd + 1], (SSD_CHUNK, SSD_CHUNK)))
                decay = jnp.where(causal, jnp.exp(bcc - c_t[hd:hd + 1, :]), 0.0)
                m_blocks.append((cb * decay).astype(BF16))
            cexp = jnp.concatenate([jnp.where(low_half, bc_c[0], bc_c[1]),
                                    jnp.where(low_half, bc_c[2], bc_c[3])], axis=1)
            dtexp = jnp.concatenate([jnp.where(low_half, bc_dt[0], bc_dt[1]),
                                     jnp.where(low_half, bc_dt[2], bc_dt[3])], axis=1)
            xdt = xg * dtexp
            xdt_b = xdt.astype(BF16)
            zero_b = jnp.zeros_like(xdt_b)
            rhs = jnp.concatenate(
                [jnp.where(lane_g // SSD_HEAD_DIM == hh, xdt_b, zero_b) for hh in range(HEADS_PER_GROUP)],
                axis=0)
            lhs = jnp.concatenate(m_blocks, axis=1)
            y_diag = jnp.dot(lhs, rhs, preferred_element_type=F32)
            st = state_ref[:, g * GROUP_WIDTH:(g + 1) * GROUP_WIDTH]
            y_off = jnp.dot(cg, st.astype(BF16), preferred_element_type=F32) * jnp.exp(cexp)
            y = y_diag + y_off + xg * dskip_ref[:, g * GROUP_WIDTH:(g + 1) * GROUP_WIDTH]
            y_ref[pl.ds(r0, SSD_CHUNK), g * GROUP_WIDTH:(g + 1) * GROUP_WIDTH] = y
            clast = cexp[SSD_CHUNK - 1:SSD_CHUNK, :]
            xw = (xdt * jnp.exp(clast - cexp)).astype(BF16)
            upd = lax.dot_general(bg, xw, (((0,), (0,)), ((), ())), preferred_element_type=F32)
            state_ref[:, g * GROUP_WIDTH:(g + 1) * GROUP_WIDTH] = st * jnp.exp(clast) + upd
        return carry

    lax.fori_loop(0, ts // SSD_CHUNK, chunk_body, 0)

    yz = y_ref[...] * (z * _sigmoid(z))
    parts = []
    for g in range(SSD_GROUPS):
        sl = slice(g * GROUP_WIDTH, (g + 1) * GROUP_WIDTH)
        parts.append(_rms(yz[:, sl], ng_ref[:, sl]).astype(BF16))

    uprev = uhalo_ref[...]
    uhalo_ref[...] = u[ts - POOL_HALO:ts]
    uext = jnp.concatenate([uprev, u], axis=0)
    pos = (lax.broadcasted_iota(jnp.int32, (ts, POOL_GROUP), 0) + (i * ts + 1)).astype(F32)
    pooled = []
    for gi, w in enumerate(POOL_WINDOWS):
        s = uext[:, gi * POOL_GROUP:(gi + 1) * POOL_GROUP]
        sh = 1
        while sh < w:
            s = s + pltpu.roll(s, sh, axis=0)
            sh *= 2
        mean = s[POOL_HALO:] / jnp.minimum(pos, float(w))
        pooled.append((mean - u[:, gi * POOL_GROUP:(gi + 1) * POOL_GROUP]).astype(BF16))
    mixed = jnp.dot(jnp.concatenate(pooled, axis=1), pw_ref[...], preferred_element_type=F32) * ps_ref[...]
    parts.append(mixed.astype(BF16))

    ymix = jnp.concatenate(parts, axis=1)
    o_ref[...] = h + jnp.dot(ymix, wout_ref[...], preferred_element_type=F32)


def _resident(shape):
    nd = len(shape)
    return pl.BlockSpec(shape, lambda b, i: (0,) * nd, pipeline_mode=pl.Buffered(1))


def _mixer_call(h, g, win, cw, cb, dtb, alog, dskip, ng, pw, ps, wout, ts):
    bsz, seq, d = h.shape
    tile = pl.BlockSpec((None, ts, d), lambda b, i: (b, i, 0))
    weights = (g, win, cw, cb, dtb, alog, dskip, ng, pw, ps, wout)
    return pl.pallas_call(
        _mixer_kernel,
        out_shape=jax.ShapeDtypeStruct(h.shape, F32),
        grid=(bsz, seq // ts),
        in_specs=[tile] + [_resident(w.shape) for w in weights],
        out_specs=tile,
        scratch_shapes=[
            pltpu.VMEM((SSD_STATE, SSD_WIDTH), F32),
            pltpu.VMEM((SUBLANES, SSD_XBC), F32),
            pltpu.VMEM((POOL_HALO, POOL_WIDTH), F32),
            pltpu.VMEM((ts, SSD_XBC), F32),
            pltpu.VMEM((ts, DT_PAD), F32),
            pltpu.VMEM((ts, SSD_WIDTH), F32),
        ],
        compiler_params=pltpu.CompilerParams(
            dimension_semantics=("arbitrary", "arbitrary"), vmem_limit_bytes=VMEM_LIMIT),
        name="mixer",
    )(h, *weights)


def _channel_kernel(*refs, last):
    if last:
        (h_ref, p_ref, g_ref, wup_ref, cw_ref, cb_ref, wdn_ref, gp_ref, wg_ref, wp_ref, fg_ref,
         o_ref, halo_ref, act_ref) = refs
    else:
        (h_ref, p_ref, g_ref, wup_ref, cw_ref, cb_ref, wdn_ref, gp_ref, wg_ref, wp_ref,
         o_ref, halo_ref, act_ref) = refs
    ts = h_ref.shape[0]
    i = pl.program_id(1)

    @pl.when(i == 0)
    def _():
        halo_ref[...] = jnp.zeros_like(halo_ref)

    h = h_ref[...]
    hn = _rms(h, g_ref[...]).astype(BF16)

    def conv_block(col):
        sl = slice(col, col + FF_BLOCK)
        up = jnp.dot(hn, wup_ref[:, sl], preferred_element_type=F32)
        prev = halo_ref[:, sl]
        halo_ref[:, sl] = up[ts - SUBLANES:ts]
        return _causal_conv(up, prev, cw_ref[:, sl], cb_ref[:, sl])

    for j in range(D_FF // FF_BLOCK):
        gate = conv_block(j * FF_BLOCK)
        val = conv_block(D_FF + j * FF_BLOCK)
        act_ref[:, j * FF_BLOCK:(j + 1) * FF_BLOCK] = (_gelu_tanh(gate) * val).astype(BF16)

    h1 = h + jnp.dot(act_ref[...], wdn_ref[...], preferred_element_type=F32)
    hn2 = _rms(h1, gp_ref[...]).astype(BF16)
    gatev = _sigmoid(jnp.dot(hn2, wg_ref[...], preferred_element_type=F32))
    emb = jnp.dot(p_ref[...].astype(BF16), wp_ref[...], preferred_element_type=F32)
    h2 = h1 + emb * gatev
    o_ref[...] = _rms(h2, fg_ref[...]) if last else h2


def _channel_call(h, p, g, wup, cw, cb, wdn, gp, wg, wp, fg, ts):
    bsz, seq, d = h.shape
    tile = pl.BlockSpec((None, ts, d), lambda b, i: (b, i, 0))
    ptile = pl.BlockSpec((None, ts, p.shape[-1]), lambda b, i: (b, i, 0))
    last = fg is not None
    weights = (g, wup, cw, cb, wdn, gp, wg, wp) + ((fg,) if last else ())
    return pl.pallas_call(
        functools.partial(_channel_kernel, last=last),
        out_shape=jax.ShapeDtypeStruct(h.shape, F32),
        grid=(bsz, seq // ts),
        in_specs=[tile, ptile] + [_resident(w.shape) for w in weights],
        out_specs=tile,
        scratch_shapes=[
            pltpu.VMEM((SUBLANES, 2 * D_FF), F32),
            pltpu.VMEM((ts, D_FF), BF16),
        ],
        compiler_params=pltpu.CompilerParams(
            dimension_semantics=("arbitrary", "arbitrary"), vmem_limit_bytes=VMEM_LIMIT),
        name="channel_last" if last else "channel",
    )(h, p, *weights)


def _block_diag(w):
    n, a, b = w.shape
    out = jnp.zeros((n * a, n * b), w.dtype)
    for k in range(n):
        out = out.at[k * a:(k + 1) * a, k * b:(k + 1) * b].set(w[k])
    return out


def _row(v, width=None):
    v = v.reshape(1, -1).astype(F32)
    if width is not None and v.shape[1] < width:
        v = jnp.pad(v, ((0, 0), (0, width - v.shape[1])))
    return v


def kernel(x, p, mix_norm_g, w_in, ssd_conv_w, ssd_conv_b, ssd_dt_bias, ssd_a_log, ssd_d, ssd_norm_g,
           pool_w, pool_scale, w_out, ffn_norm_g, ffn_w_up, ffn_conv_w, ffn_conv_b, ffn_w_down,
           ple_norm_g, ple_w_gate, ple_w_proj, final_norm_g):
    depth = w_in.shape[0]
    seq = x.shape[1]
    ts = min(SEQ_TILE, seq)
    assert seq % ts == 0 and ts % SSD_CHUNK == 0
    c0, c1, c2 = SSD_WIDTH, SSD_WIDTH + SSD_XBC, SSD_WIDTH + SSD_XBC + SSD_HEADS
    h = x
    for l in range(depth):
        wl = w_in[l]
        win = jnp.concatenate(
            [wl[:, :c0], wl[:, c0:c1], wl[:, c2:], jnp.pad(wl[:, c1:c2], ((0, 0), (0, DT_PAD - SSD_HEADS)))],
            axis=1).astype(BF16)
        h = _mixer_call(
            h, _row(mix_norm_g[l]), win, ssd_conv_w[l], _row(ssd_conv_b[l]),
            _row(ssd_dt_bias[l], DT_PAD), _row(ssd_a_log[l], DT_PAD),
            _row(jnp.repeat(ssd_d[l], SSD_HEAD_DIM)), _row(ssd_norm_g[l]),
            _block_diag(pool_w[l]).astype(BF16), _row(pool_scale[l]), w_out[l].astype(BF16), ts)
        h = _channel_call(
            h, p[l], _row(ffn_norm_g[l]), ffn_w_up[l].astype(BF16), ffn_conv_w[l], _row(ffn_conv_b[l]),
            ffn_w_down[l].astype(BF16), _row(ple_norm_g[l]), ple_w_gate[l].astype(BF16),
            ple_w_proj[l].astype(BF16), _row(final_norm_g) if l == depth - 1 else None, ts)
    return h
```

```python
import functools
import math

import jax
import jax.numpy as jnp
from jax import lax
from jax.experimental import pallas as pl
from jax.experimental.pallas import tpu as pltpu

F32 = jnp.float32
BF16 = jnp.bfloat16

D_MODEL = 1024
D_PLE = 256
SSD_WIDTH = 512
SSD_HEAD_DIM = 64
SSD_HEADS = 8
SSD_GROUPS = 2
SSD_STATE = 128
SSD_CONV = 4
SSD_CHUNK = 128
SSD_XBC = SSD_WIDTH + 2 * SSD_GROUPS * SSD_STATE
GROUP_WIDTH = SSD_WIDTH // SSD_GROUPS
HEADS_PER_GROUP = SSD_HEADS // SSD_GROUPS
POOL_WINDOWS = (2, 4, 8, 16)
POOL_WIDTH = 512
POOL_GROUP = 128
D_FF = 2816
FFN_CONV = 3
EPS = 1e-6

LANES = 128
SUBLANES = 8
DT_PAD = LANES
D_PROJ = SSD_WIDTH + SSD_XBC + POOL_WIDTH + DT_PAD
CONV_HALO = SUBLANES
POOL_HALO = 2 * SUBLANES
FF_BLOCK = 256
SEQ_TILE = 512
VMEM_LIMIT = 56 * 1024 * 1024


def _rms(x, g):
    ms = jnp.mean(x * x, axis=-1, keepdims=True)
    return x * lax.rsqrt(ms + EPS) * g


def _sigmoid(x):
    return 1.0 / (1.0 + jnp.exp(-x))


def _softplus(x):
    return jnp.maximum(x, 0.0) + jnp.log(1.0 + jnp.exp(-jnp.abs(x)))


def _gelu_tanh(x):
    c = math.sqrt(2.0 / math.pi)
    return x * (0.5 * (1.0 + jnp.tanh(c * (x + 0.044715 * (x * x * x)))))


def _history_push(hist_ref, col, v, halo):
    hist_ref[col, halo:halo + v.shape[0], :] = v


def _history_roll(hist_ref, col, ts, halo):
    hist_ref[col, 0:halo, :] = hist_ref[col, ts:ts + halo, :]


def _causal_conv_col(hist_ref, col, v, w, b):
    ts = v.shape[0]
    taps = w.shape[0]
    _history_push(hist_ref, col, v, CONV_HALO)
    acc = b + v * w[taps - 1:taps]
    for j in range(1, taps):
        acc = acc + hist_ref[col, pl.ds(CONV_HALO - j, ts), :] * w[taps - 1 - j:taps - j]
    _history_roll(hist_ref, col, ts, CONV_HALO)
    return acc


def _ssd_chunk(xcc, dtc, st, a_row, dskip, consts):
    causal, tri, low_half, head_masks = consts
    q = SSD_CHUNK
    ccol = jnp.dot(tri, dtc * a_row, precision=lax.Precision.HIGHEST,
                   preferred_element_type=F32)
    c_t = ccol.T
    ys, new_st = [], []
    for g in range(SSD_GROUPS):
        gs = slice(g * GROUP_WIDTH, (g + 1) * GROUP_WIDTH)
        xg = xcc[:, gs]
        b_off = SSD_WIDTH + g * SSD_STATE
        c_off = SSD_WIDTH + SSD_GROUPS * SSD_STATE + g * SSD_STATE
        bg = xcc[:, b_off:b_off + SSD_STATE].astype(BF16)
        cg = xcc[:, c_off:c_off + SSD_STATE].astype(BF16)
        cb = lax.dot_general(cg, bg, (((1,), (1,)), ((), ())), preferred_element_type=F32)
        bc_c, bc_dt, m_blocks = [], [], []
        for hh in range(HEADS_PER_GROUP):
            hd = g * HEADS_PER_GROUP + hh
            bcc = jnp.broadcast_to(ccol[:, hd:hd + 1], (q, q))
            bc_c.append(bcc)
            bc_dt.append(jnp.broadcast_to(dtc[:, hd:hd + 1], (q, q)))
            decay = jnp.where(causal, jnp.exp(bcc - c_t[hd:hd + 1, :]), 0.0)
            m_blocks.append((cb * decay).astype(BF16))
        cexp = jnp.concatenate([jnp.where(low_half, bc_c[0], bc_c[1]),
                                jnp.where(low_half, bc_c[2], bc_c[3])], axis=1)
        dtexp = jnp.concatenate([jnp.where(low_half, bc_dt[0], bc_dt[1]),
                                 jnp.where(low_half, bc_dt[2], bc_dt[3])], axis=1)
        xdt = xg * dtexp
        xdt_b = xdt.astype(BF16)
        zero_b = jnp.zeros_like(xdt_b)
        rhs = jnp.concatenate([jnp.where(m, xdt_b, zero_b) for m in head_masks], axis=0)
        y_diag = jnp.dot(jnp.concatenate(m_blocks, axis=1), rhs, preferred_element_type=F32)
        stg = st[:, gs]
        y_off = jnp.dot(cg, stg.astype(BF16), preferred_element_type=F32) * jnp.exp(cexp)
        ys.append(y_diag + y_off + xg * dskip[:, gs])
        clast = cexp[q - 1:q, :]
        xw = (xdt * jnp.exp(clast - cexp)).astype(BF16)
        upd = lax.dot_general(bg, xw, (((0,), (0,)), ((), ())), preferred_element_type=F32)
        new_st.append(stg * jnp.exp(clast) + upd)
    return jnp.concatenate(ys, axis=1), jnp.concatenate(new_st, axis=1)


def _mixer_kernel(h_ref, g_ref, win_ref, cw_ref, cb_ref, dtb_ref, alog_ref, dskip_ref, ng_ref,
                  pw_ref, ps_ref, wout_ref, o_ref,
                  state_ref, xhist_ref, uhist_ref):
    ts = h_ref.shape[0]
    i = pl.program_id(1)

    @pl.when(i == 0)
    def _():
        state_ref[...] = jnp.zeros_like(state_ref)
        xhist_ref[:, 0:CONV_HALO, :] = jnp.zeros((xhist_ref.shape[0], CONV_HALO, LANES), F32)
        uhist_ref[:, 0:POOL_HALO, :] = jnp.zeros((uhist_ref.shape[0], POOL_HALO, LANES), F32)

    h = h_ref[...]
    hn = _rms(h, g_ref[...]).astype(BF16)
    proj = jnp.dot(hn, win_ref[...], preferred_element_type=F32)
    z = proj[:, 0:SSD_WIDTH]
    u_off = SSD_WIDTH + SSD_XBC
    dt = _softplus(proj[:, D_PROJ - DT_PAD:D_PROJ] + dtb_ref[...])

    cols = []
    for j in range(SSD_XBC // LANES):
        sl = slice(j * LANES, (j + 1) * LANES)
        v = proj[:, SSD_WIDTH + j * LANES:SSD_WIDTH + (j + 1) * LANES]
        xconv = _causal_conv_col(xhist_ref, j, v, cw_ref[:, sl], cb_ref[:, sl])
        cols.append(xconv * _sigmoid(xconv))
    xc = jnp.concatenate(cols, axis=1)

    a_row = -jnp.exp(alog_ref[...])
    rows = lax.broadcasted_iota(jnp.int32, (SSD_CHUNK, SSD_CHUNK), 0)
    lanes = lax.broadcasted_iota(jnp.int32, (SSD_CHUNK, SSD_CHUNK), 1)
    causal = rows >= lanes
    lane_g = lax.broadcasted_iota(jnp.int32, (SSD_CHUNK, GROUP_WIDTH), 1)
    head_masks = [(lane_g >= hh * SSD_HEAD_DIM) & (lane_g < (hh + 1) * SSD_HEAD_DIM)
                  for hh in range(HEADS_PER_GROUP)]
    consts = (causal, causal.astype(F32), lanes < SSD_HEAD_DIM, head_masks)

    st = state_ref[...]
    ys = []
    for c in range(ts // SSD_CHUNK):
        rs = slice(c * SSD_CHUNK, (c + 1) * SSD_CHUNK)
        y, st = _ssd_chunk(xc[rs], dt[rs], st, a_row, dskip_ref[...], consts)
        ys.append(y)
    state_ref[...] = st

    yz = jnp.concatenate(ys, axis=0) * (z * _sigmoid(z))
    parts = []
    for g in range(SSD_GROUPS):
        sl = slice(g * GROUP_WIDTH, (g + 1) * GROUP_WIDTH)
        parts.append(_rms(yz[:, sl], ng_ref[:, sl]).astype(BF16))

    pos_head = (lax.broadcasted_iota(jnp.int32, (POOL_HALO, POOL_GROUP), 0) + (i * ts + 1)).astype(F32)
    pooled = []
    for gi, w in enumerate(POOL_WINDOWS):
        v = proj[:, u_off + gi * POOL_GROUP:u_off + (gi + 1) * POOL_GROUP]
        _history_push(uhist_ref, gi, v, POOL_HALO)
        s = v
        for k in range(1, w):
            s = s + uhist_ref[gi, pl.ds(POOL_HALO - k, ts), :]
        _history_roll(uhist_ref, gi, ts, POOL_HALO)
        inv = jnp.concatenate([1.0 / jnp.minimum(pos_head, float(w)),
                               jnp.full((ts - POOL_HALO, POOL_GROUP), 1.0 / w, F32)], axis=0)
        pooled.append((s * inv - v).astype(BF16))
    mixed = jnp.dot(jnp.concatenate(pooled, axis=1), pw_ref[...], preferred_element_type=F32) * ps_ref[...]
    parts.append(mixed.astype(BF16))

    ymix = jnp.concatenate(parts, axis=1)
    o_ref[...] = h + jnp.dot(ymix, wout_ref[...], preferred_element_type=F32)


def _layer_spec(arr, layer):
    nd = arr.ndim - 1
    return pl.BlockSpec((None,) + arr.shape[1:], lambda b, i: (layer,) + (0,) * nd,
                        pipeline_mode=pl.Buffered(1))


def _mixer_call(h, layer, weights, ts):
    bsz, seq, d = h.shape
    tile = pl.BlockSpec((None, ts, d), lambda b, i: (b, i, 0))
    return pl.pallas_call(
        _mixer_kernel,
        out_shape=jax.ShapeDtypeStruct(h.shape, F32),
        grid=(bsz, seq // ts),
        in_specs=[tile] + [_layer_spec(w, layer) for w in weights],
        out_specs=tile,
        scratch_shapes=[
            pltpu.VMEM((SSD_STATE, SSD_WIDTH), F32),
            pltpu.VMEM((SSD_XBC // LANES, CONV_HALO + ts, LANES), F32),
            pltpu.VMEM((POOL_WIDTH // LANES, POOL_HALO + ts, LANES), F32),
        ],
        compiler_params=pltpu.CompilerParams(
            dimension_semantics=("arbitrary", "arbitrary"), vmem_limit_bytes=VMEM_LIMIT),
        name="mixer",
    )(h, *weights)


def _channel_kernel(*refs, last):
    if last:
        (h_ref, p_ref, g_ref, wup_ref, cw_ref, cb_ref, wdn_ref, gp_ref, wg_ref, wp_ref, fg_ref,
         o_ref, hist_ref, act_ref) = refs
    else:
        (h_ref, p_ref, g_ref, wup_ref, cw_ref, cb_ref, wdn_ref, gp_ref, wg_ref, wp_ref,
         o_ref, hist_ref, act_ref) = refs
    ts = h_ref.shape[0]
    i = pl.program_id(1)

    @pl.when(i == 0)
    def _():
        hist_ref[:, 0:CONV_HALO, :] = jnp.zeros((hist_ref.shape[0], CONV_HALO, LANES), F32)

    h = h_ref[...]
    hn = _rms(h, g_ref[...]).astype(BF16)

    def conv_block(col):
        up = jnp.dot(hn, wup_ref[:, col:col + FF_BLOCK], preferred_element_type=F32)
        outs = []
        for jj in range(FF_BLOCK // LANES):
            sl = slice(col + jj * LANES, col + (jj + 1) * LANES)
            outs.append(_causal_conv_col(hist_ref, col // LANES + jj, up[:, jj * LANES:(jj + 1) * LANES],
                                         cw_ref[:, sl], cb_ref[:, sl]))
        return jnp.concatenate(outs, axis=1)

    for j in range(D_FF // FF_BLOCK):
        gate = conv_block(j * FF_BLOCK)
        val = conv_block(D_FF + j * FF_BLOCK)
        act_ref[:, j * FF_BLOCK:(j + 1) * FF_BLOCK] = (_gelu_tanh(gate) * val).astype(BF16)

    h1 = h + jnp.dot(act_ref[...], wdn_ref[...], preferred_element_type=F32)
    hn2 = _rms(h1, gp_ref[...]).astype(BF16)
    gatev = _sigmoid(jnp.dot(hn2, wg_ref[...], preferred_element_type=F32))
    emb = jnp.dot(p_ref[...].astype(BF16), wp_ref[...], preferred_element_type=F32)
    h2 = h1 + emb * gatev
    o_ref[...] = _rms(h2, fg_ref[...]) if last else h2


def _channel_call(h, p, layer, weights, fg, ts):
    bsz, seq, d = h.shape
    tile = pl.BlockSpec((None, ts, d), lambda b, i: (b, i, 0))
    ptile = pl.BlockSpec((None, None, ts, p.shape[-1]), lambda b, i: (layer, b, i, 0))
    last = fg is not None
    in_specs = [tile, ptile] + [_layer_spec(w, layer) for w in weights]
    args = (h, p) + tuple(weights)
    if last:
        in_specs.append(pl.BlockSpec(fg.shape, lambda b, i: (0, 0), pipeline_mode=pl.Buffered(1)))
        args = args + (fg,)
    return pl.pallas_call(
        functools.partial(_channel_kernel, last=last),
        out_shape=jax.ShapeDtypeStruct(h.shape, F32),
        grid=(bsz, seq // ts),
        in_specs=in_specs,
        out_specs=tile,
        scratch_shapes=[
            pltpu.VMEM((2 * D_FF // LANES, CONV_HALO + ts, LANES), F32),
            pltpu.VMEM((ts, D_FF), BF16),
        ],
        compiler_params=pltpu.CompilerParams(
            dimension_semantics=("arbitrary", "arbitrary"), vmem_limit_bytes=VMEM_LIMIT),
        name="channel_last" if last else "channel",
    )(*args)


def _rows(v, width=None):
    v = v.astype(F32)[:, None, :]
    if width is not None and v.shape[-1] < width:
        v = jnp.pad(v, ((0, 0), (0, 0), (0, width - v.shape[-1])))
    return v


def _block_diag(w):
    depth, n, a, b = w.shape
    out = jnp.zeros((depth, n * a, n * b), w.dtype)
    for k in range(n):
        out = out.at[:, k * a:(k + 1) * a, k * b:(k + 1) * b].set(w[:, k])
    return out


def kernel(x, p, mix_norm_g, w_in, ssd_conv_w, ssd_conv_b, ssd_dt_bias, ssd_a_log, ssd_d, ssd_norm_g,
           pool_w, pool_scale, w_out, ffn_norm_g, ffn_w_up, ffn_conv_w, ffn_conv_b, ffn_w_down,
           ple_norm_g, ple_w_gate, ple_w_proj, final_norm_g):
    depth = w_in.shape[0]
    seq = x.shape[1]
    ts = min(SEQ_TILE, seq)
    assert seq % ts == 0 and ts % SSD_CHUNK == 0 and ts > POOL_HALO
    c0, c1, c2 = SSD_WIDTH, SSD_WIDTH + SSD_XBC, SSD_WIDTH + SSD_XBC + SSD_HEADS
    win = jnp.concatenate(
        [w_in[:, :, :c0], w_in[:, :, c0:c1], w_in[:, :, c2:],
         jnp.pad(w_in[:, :, c1:c2], ((0, 0), (0, 0), (0, DT_PAD - SSD_HEADS)))], axis=2).astype(BF16)
    mixer_w = (_rows(mix_norm_g), win, ssd_conv_w.astype(F32), _rows(ssd_conv_b),
               _rows(ssd_dt_bias, DT_PAD), _rows(ssd_a_log, DT_PAD),
               _rows(jnp.repeat(ssd_d, SSD_HEAD_DIM, axis=1)), _rows(ssd_norm_g),
               _block_diag(pool_w).astype(BF16), _rows(pool_scale), w_out.astype(BF16))
    channel_w = (_rows(ffn_norm_g), ffn_w_up.astype(BF16), ffn_conv_w.astype(F32), _rows(ffn_conv_b),
                 ffn_w_down.astype(BF16), _rows(ple_norm_g), ple_w_gate.astype(BF16), ple_w_proj.astype(BF16))
    fg = final_norm_g.astype(F32).reshape(1, -1)
    h = x
    for l in range(depth):
        h = _mixer_call(h, l, mixer_w, ts)
        h = _channel_call(h, p, l, channel_w, fg if l == depth - 1 else None, ts)
    return h
```

```python
import functools
import math

import jax
import jax.numpy as jnp
from jax import lax
from jax.experimental import pallas as pl
from jax.experimental.pallas import tpu as pltpu

F32 = jnp.float32
BF16 = jnp.bfloat16

D_MODEL = 1024
D_PLE = 256
SSD_WIDTH = 512
SSD_HEAD_DIM = 64
SSD_HEADS = 8
SSD_GROUPS = 2
SSD_STATE = 128
SSD_CONV = 4
SSD_CHUNK = 128
SSD_XBC = SSD_WIDTH + 2 * SSD_GROUPS * SSD_STATE
GROUP_WIDTH = SSD_WIDTH // SSD_GROUPS
HEADS_PER_GROUP = SSD_HEADS // SSD_GROUPS
POOL_WINDOWS = (2, 4, 8, 16)
POOL_WIDTH = 512
POOL_GROUP = 128
D_FF = 2816
FFN_CONV = 3
EPS = 1e-6

LANES = 128
SUBLANES = 8
DT_PAD = LANES
D_PROJ = SSD_WIDTH + SSD_XBC + POOL_WIDTH + DT_PAD
CONV_HALO = SUBLANES
POOL_HALO = 2 * SUBLANES
FF_BLOCK = 256
SEQ_TILE = 512
VMEM_LIMIT = 56 * 1024 * 1024


def _rms(x, g):
    ms = jnp.mean(x * x, axis=-1, keepdims=True)
    return x * lax.rsqrt(ms + EPS) * g


def _sigmoid(x):
    return 1.0 / (1.0 + jnp.exp(-x))


def _softplus(x):
    return jnp.maximum(x, 0.0) + jnp.log(1.0 + jnp.exp(-jnp.abs(x)))


def _gelu_tanh(x):
    c = math.sqrt(2.0 / math.pi)
    return x * (0.5 * (1.0 + jnp.tanh(c * (x + 0.044715 * (x * x * x)))))


def _history_push(hist_ref, col, v, halo):
    hist_ref[col, halo:halo + v.shape[0], :] = v


def _history_roll(hist_ref, col, ts, halo):
    hist_ref[col, 0:halo, :] = hist_ref[col, ts:ts + halo, :]


def _causal_conv_col(hist_ref, col, v, w, b):
    ts = v.shape[0]
    taps = w.shape[0]
    _history_push(hist_ref, col, v, CONV_HALO)
    acc = b + v * w[taps - 1:taps]
    for j in range(1, taps):
        acc = acc + hist_ref[col, pl.ds(CONV_HALO - j, ts), :] * w[taps - 1 - j:taps - j]
    _history_roll(hist_ref, col, ts, CONV_HALO)
    return acc


def _ssd_pre(xcc, dt_raw, dtb, a_heads, tri_upper):
    dt_t = _softplus(dt_raw.T[0:SSD_HEADS] + dtb)
    c_t = jnp.dot(dt_t * a_heads, tri_upper, precision=lax.Precision.HIGHEST,
                  preferred_element_type=F32)
    pad = jnp.zeros((SSD_CHUNK - 2 * SSD_HEADS, SSD_CHUNK), F32)
    cols = jnp.concatenate([c_t, dt_t, pad], axis=0).T
    groups = []
    for g in range(SSD_GROUPS):
        b_off = SSD_WIDTH + g * SSD_STATE
        c_off = SSD_WIDTH + SSD_GROUPS * SSD_STATE + g * SSD_STATE
        bg = xcc[:, b_off:b_off + SSD_STATE].astype(BF16)
        cg = xcc[:, c_off:c_off + SSD_STATE].astype(BF16)
        cb = lax.dot_general(cg, bg, (((1,), (1,)), ((), ())), preferred_element_type=F32)
        groups.append((bg, cg, cb))
    return cols, c_t, groups


def _ssd_chunk(xcc, st, pre, dskip, consts):
    causal, low_half, head_masks = consts
    cols, c_t, groups = pre
    q = SSD_CHUNK
    ys, new_st = [], []
    for g in range(SSD_GROUPS):
        gs = slice(g * GROUP_WIDTH, (g + 1) * GROUP_WIDTH)
        xg = xcc[:, gs]
        bg, cg, cb = groups[g]
        bc_c, bc_dt, m_blocks = [], [], []
        for hh in range(HEADS_PER_GROUP):
            hd = g * HEADS_PER_GROUP + hh
            bcc = jnp.broadcast_to(cols[:, hd:hd + 1], (q, q))
            bc_c.append(bcc)
            bc_dt.append(jnp.broadcast_to(cols[:, SSD_HEADS + hd:SSD_HEADS + hd + 1], (q, q)))
            decay = jnp.where(causal, jnp.exp(bcc - c_t[hd:hd + 1, :]), 0.0)
            m_blocks.append((cb * decay).astype(BF16))
        cexp = jnp.concatenate([jnp.where(low_half, bc_c[0], bc_c[1]),
                                jnp.where(low_half, bc_c[2], bc_c[3])], axis=1)
        dtexp = jnp.concatenate([jnp.where(low_half, bc_dt[0], bc_dt[1]),
                                 jnp.where(low_half, bc_dt[2], bc_dt[3])], axis=1)
        xdt = xg * dtexp
        xdt_b = xdt.astype(BF16)
        zero_b = jnp.zeros_like(xdt_b)
        rhs = jnp.concatenate([jnp.where(m, xdt_b, zero_b) for m in head_masks], axis=0)
        y_diag = jnp.dot(jnp.concatenate(m_blocks, axis=1), rhs, preferred_element_type=F32)
        stg = st[:, gs]
        y_off = jnp.dot(cg, stg.astype(BF16), preferred_element_type=F32) * jnp.exp(cexp)
        ys.append(y_diag + y_off + xg * dskip[:, gs])
        clast = cexp[q - 1:q, :]
        xw = (xdt * jnp.exp(clast - cexp)).astype(BF16)
        upd = lax.dot_general(bg, xw, (((0,), (0,)), ((), ())), preferred_element_type=F32)
        new_st.append(stg * jnp.exp(clast) + upd)
    return jnp.concatenate(ys, axis=1), jnp.concatenate(new_st, axis=1)


def _in_proj(h, g, win):
    return jnp.dot(_rms(h, g).astype(BF16), win, preferred_element_type=F32)


def _mixer_kernel(h_ref, hnext_ref, g_ref, win_ref, cw_ref, cb_ref, dtb_ref, alog_ref, dskip_ref, ng_ref,
                  pw_ref, ps_ref, wout_ref, o_ref,
                  state_ref, xhist_ref, uhist_ref, proj_ref):
    ts = h_ref.shape[0]
    i = pl.program_id(1)

    @pl.when(i == 0)
    def _():
        state_ref[...] = jnp.zeros_like(state_ref)
        xhist_ref[:, 0:CONV_HALO, :] = jnp.zeros((xhist_ref.shape[0], CONV_HALO, LANES), F32)
        uhist_ref[:, 0:POOL_HALO, :] = jnp.zeros((uhist_ref.shape[0], POOL_HALO, LANES), F32)
        proj_ref[...] = _in_proj(h_ref[...], g_ref[...], win_ref[...])

    z = proj_ref[:, 0:SSD_WIDTH]
    zgate = z * _sigmoid(z)
    u_off = SSD_WIDTH + SSD_XBC
    dt_raw = proj_ref[:, D_PROJ - DT_PAD:D_PROJ]

    cols = []
    for j in range(SSD_XBC // LANES):
        sl = slice(j * LANES, (j + 1) * LANES)
        v = proj_ref[:, SSD_WIDTH + j * LANES:SSD_WIDTH + (j + 1) * LANES]
        xconv = _causal_conv_col(xhist_ref, j, v, cw_ref[:, sl], cb_ref[:, sl])
        cols.append(xconv * _sigmoid(xconv))
    xc = jnp.concatenate(cols, axis=1)

    pos_head = (lax.broadcasted_iota(jnp.int32, (POOL_HALO, POOL_GROUP), 0) + (i * ts + 1)).astype(F32)
    pooled = []
    for gi, w in enumerate(POOL_WINDOWS):
        v = proj_ref[:, u_off + gi * POOL_GROUP:u_off + (gi + 1) * POOL_GROUP]
        _history_push(uhist_ref, gi, v, POOL_HALO)
        s = v
        for k in range(1, w):
            s = s + uhist_ref[gi, pl.ds(POOL_HALO - k, ts), :]
        _history_roll(uhist_ref, gi, ts, POOL_HALO)
        inv = jnp.concatenate([1.0 / jnp.minimum(pos_head, float(w)),
                               jnp.full((ts - POOL_HALO, POOL_GROUP), 1.0 / w, F32)], axis=0)
        pooled.append((s * inv - v).astype(BF16))

    a_heads = -jnp.exp(alog_ref[...])
    rows = lax.broadcasted_iota(jnp.int32, (SSD_CHUNK, SSD_CHUNK), 0)
    lanes = lax.broadcasted_iota(jnp.int32, (SSD_CHUNK, SSD_CHUNK), 1)
    causal = rows >= lanes
    lane_g = lax.broadcasted_iota(jnp.int32, (SSD_CHUNK, GROUP_WIDTH), 1)
    head_masks = [(lane_g >= hh * SSD_HEAD_DIM) & (lane_g < (hh + 1) * SSD_HEAD_DIM)
                  for hh in range(HEADS_PER_GROUP)]
    consts = (causal, lanes < SSD_HEAD_DIM, head_masks)

    nchunks = ts // SSD_CHUNK
    chunk_rows = [slice(c * SSD_CHUNK, (c + 1) * SSD_CHUNK) for c in range(nchunks)]
    tri_upper = (rows <= lanes).astype(F32)
    pre = [_ssd_pre(xc[rs], dt_raw[rs], dtb_ref[...], a_heads, tri_upper) for rs in chunk_rows]

    hn_next = _rms(hnext_ref[...], g_ref[...]).astype(BF16)
    mxu_cols = 2 * LANES
    bounds = [(D_PROJ // mxu_cols) * c // nchunks * mxu_cols for c in range(nchunks)] + [D_PROJ]

    st = state_ref[...]
    ys = []
    for c, rs in enumerate(chunk_rows):
        y, st = _ssd_chunk(xc[rs], st, pre[c], dskip_ref[...], consts)
        ys.append(y)
        lo, hi = bounds[c], bounds[c + 1]
        proj_ref[:, lo:hi] = jnp.dot(hn_next, win_ref[:, lo:hi], preferred_element_type=F32)
    state_ref[...] = st

    yz = jnp.concatenate(ys, axis=0) * zgate
    parts = []
    for g in range(SSD_GROUPS):
        sl = slice(g * GROUP_WIDTH, (g + 1) * GROUP_WIDTH)
        parts.append(_rms(yz[:, sl], ng_ref[:, sl]).astype(BF16))
    mixed = jnp.dot(jnp.concatenate(pooled, axis=1), pw_ref[...], preferred_element_type=F32) * ps_ref[...]
    parts.append(mixed.astype(BF16))

    ymix = jnp.concatenate(parts, axis=1)
    o_ref[...] = h_ref[...] + jnp.dot(ymix, wout_ref[...], preferred_element_type=F32)


def _layer_spec(arr, layer):
    nd = arr.ndim - 1
    return pl.BlockSpec((None,) + arr.shape[1:], lambda b, i: (layer,) + (0,) * nd,
                        pipeline_mode=pl.Buffered(1))


def _mixer_call(h, layer, weights, ts):
    bsz, seq, d = h.shape
    nt = seq // ts
    tile = pl.BlockSpec((None, ts, d), lambda b, i: (b, i, 0))
    next_tile = pl.BlockSpec((None, ts, d), lambda b, i: (b, jnp.minimum(i + 1, nt - 1), 0))
    return pl.pallas_call(
        _mixer_kernel,
        out_shape=jax.ShapeDtypeStruct(h.shape, F32),
        grid=(bsz, nt),
        in_specs=[tile, next_tile] + [_layer_spec(w, layer) for w in weights],
        out_specs=tile,
        scratch_shapes=[
            pltpu.VMEM((SSD_STATE, SSD_WIDTH), F32),
            pltpu.VMEM((SSD_XBC // LANES, CONV_HALO + ts, LANES), F32),
            pltpu.VMEM((POOL_WIDTH // LANES, POOL_HALO + ts, LANES), F32),
            pltpu.VMEM((ts, D_PROJ), F32),
        ],
        compiler_params=pltpu.CompilerParams(
            dimension_semantics=("arbitrary", "arbitrary"), vmem_limit_bytes=VMEM_LIMIT),
        name="mixer",
    )(h, h, *weights)


def _channel_kernel(*refs, last):
    if last:
        (h_ref, p_ref, g_ref, wup_ref, cw_ref, cb_ref, wdn_ref, gp_ref, wg_ref, wp_ref, fg_ref,
         o_ref, hist_ref, act_ref) = refs
    else:
        (h_ref, p_ref, g_ref, wup_ref, cw_ref, cb_ref, wdn_ref, gp_ref, wg_ref, wp_ref,
         o_ref, hist_ref, act_ref) = refs
    ts = h_ref.shape[0]
    i = pl.program_id(1)

    @pl.when(i == 0)
    def _():
        hist_ref[:, 0:CONV_HALO, :] = jnp.zeros((hist_ref.shape[0], CONV_HALO, LANES), F32)

    h = h_ref[...]
    hn = _rms(h, g_ref[...]).astype(BF16)

    def conv_block(col):
        up = jnp.dot(hn, wup_ref[:, col:col + FF_BLOCK], preferred_element_type=F32)
        outs = []
        for jj in range(FF_BLOCK // LANES):
            sl = slice(col + jj * LANES, col + (jj + 1) * LANES)
            outs.append(_causal_conv_col(hist_ref, col // LANES + jj, up[:, jj * LANES:(jj + 1) * LANES],
                                         cw_ref[:, sl], cb_ref[:, sl]))
        return jnp.concatenate(outs, axis=1)

    for j in range(D_FF // FF_BLOCK):
        gate = conv_block(j * FF_BLOCK)
        val = conv_block(D_FF + j * FF_BLOCK)
        act_ref[:, j * FF_BLOCK:(j + 1) * FF_BLOCK] = (_gelu_tanh(gate) * val).astype(BF16)

    h1 = h + jnp.dot(act_ref[...], wdn_ref[...], preferred_element_type=F32)
    hn2 = _rms(h1, gp_ref[...]).astype(BF16)
    gatev = _sigmoid(jnp.dot(hn2, wg_ref[...], preferred_element_type=F32))
    emb = jnp.dot(p_ref[...].astype(BF16), wp_ref[...], preferred_element_type=F32)
    h2 = h1 + emb * gatev
    o_ref[...] = _rms(h2, fg_ref[...]) if last else h2


def _channel_call(h, p, layer, weights, fg, ts):
    bsz, seq, d = h.shape
    tile = pl.BlockSpec((None, ts, d), lambda b, i: (b, i, 0))
    ptile = pl.BlockSpec((None, None, ts, p.shape[-1]), lambda b, i: (layer, b, i, 0))
    last = fg is not None
    in_specs = [tile, ptile] + [_layer_spec(w, layer) for w in weights]
    args = (h, p) + tuple(weights)
    if last:
        in_specs.append(pl.BlockSpec(fg.shape, lambda b, i: (0, 0), pipeline_mode=pl.Buffered(1)))
        args = args + (fg,)
    return pl.pallas_call(
        functools.partial(_channel_kernel, last=last),
        out_shape=jax.ShapeDtypeStruct(h.shape, F32),
        grid=(bsz, seq // ts),
        in_specs=in_specs,
        out_specs=tile,
        scratch_shapes=[
            pltpu.VMEM((2 * D_FF // LANES, CONV_HALO + ts, LANES), F32),
            pltpu.VMEM((ts, D_FF), BF16),
        ],
        compiler_params=pltpu.CompilerParams(
            dimension_semantics=("arbitrary", "arbitrary"), vmem_limit_bytes=VMEM_LIMIT),
        name="channel_last" if last else "channel",
    )(*args)


def _rows(v, width=None):
    v = v.astype(F32)[:, None, :]
    if width is not None and v.shape[-1] < width:
        v = jnp.pad(v, ((0, 0), (0, 0), (0, width - v.shape[-1])))
    return v


def _head_rows(v):
    return jnp.broadcast_to(v.astype(F32)[:, :, None], v.shape + (SSD_CHUNK,))


def _block_diag(w):
    depth, n, a, b = w.shape
    out = jnp.zeros((depth, n * a, n * b), w.dtype)
    for k in range(n):
        out = out.at[:, k * a:(k + 1) * a, k * b:(k + 1) * b].set(w[:, k])
    return out


def kernel(x, p, mix_norm_g, w_in, ssd_conv_w, ssd_conv_b, ssd_dt_bias, ssd_a_log, ssd_d, ssd_norm_g,
           pool_w, pool_scale, w_out, ffn_norm_g, ffn_w_up, ffn_conv_w, ffn_conv_b, ffn_w_down,
           ple_norm_g, ple_w_gate, ple_w_proj, final_norm_g):
    depth = w_in.shape[0]
    seq = x.shape[1]
    ts = min(SEQ_TILE, seq)
    assert seq % ts == 0 and ts % SSD_CHUNK == 0 and ts > POOL_HALO
    c0, c1, c2 = SSD_WIDTH, SSD_WIDTH + SSD_XBC, SSD_WIDTH + SSD_XBC + SSD_HEADS
    win = jnp.concatenate(
        [w_in[:, :, :c0], w_in[:, :, c0:c1], w_in[:, :, c2:],
         jnp.pad(w_in[:, :, c1:c2], ((0, 0), (0, 0), (0, DT_PAD - SSD_HEADS)))], axis=2).astype(BF16)
    mixer_w = (_rows(mix_norm_g), win, ssd_conv_w.astype(F32), _rows(ssd_conv_b),
               _head_rows(ssd_dt_bias), _head_rows(ssd_a_log),
               _rows(jnp.repeat(ssd_d, SSD_HEAD_DIM, axis=1)), _rows(ssd_norm_g),
               _block_diag(pool_w).astype(BF16), _rows(pool_scale), w_out.astype(BF16))
    channel_w = (_rows(ffn_norm_g), ffn_w_up.astype(BF16), ffn_conv_w.astype(F32), _rows(ffn_conv_b),
                 ffn_w_down.astype(BF16), _rows(ple_norm_g), ple_w_gate.astype(BF16), ple_w_proj.astype(BF16))
    fg = final_norm_g.astype(F32).reshape(1, -1)
    h = x
    for l in range(depth):
        h = _mixer_call(h, l, mixer_w, ts)
        h = _channel_call(h, p, l, channel_w, fg if l == depth - 1 else None, ts)
    return h
```
